```python
import math
import jax, jax.numpy as jnp
from jax import lax
import numpy as np

D_MODEL = 1024
BATCH = 32
SEQ = 2048
DEPTH = 4

GRID_W = 64
CTX_LEN = 256
ROPE_THETA = 10000.0
EPS = 1e-6
NEG_INF = -1e30
Q_BLOCK = 128
D_FF = 4 * D_MODEL

MLA_HEADS = 8
MLA_NOPE = 64
MLA_ROPE = 32
MLA_V = 64
MLA_Q_RANK = 256
MLA_KV_RANK = 128
MLA_SCALE = (MLA_NOPE + MLA_ROPE) ** -0.5
DIFF_HEADS = 4
DIFF_HD = 64
DIFF_SCALE = DIFF_HD ** -0.5
GQA_HEADS = 8
GQA_KV_HEADS = 2
GQA_GROUP = GQA_HEADS // GQA_KV_HEADS
GQA_HD = 64
NA_HEADS = 8
NA_HD = 64
NA_WIN_ROWS = 8
NA_WIN_COLS = 16
NA_COL_CHUNK = 16
NA_COL_BAND = NA_COL_CHUNK + NA_WIN_COLS
HEAD_SCALE = GQA_HD ** -0.5

EVEN_SPLITS = (MLA_Q_RANK, MLA_KV_RANK, MLA_ROPE,
               DIFF_HEADS * 2 * DIFF_HD, DIFF_HEADS * 2 * DIFF_HD, DIFF_HEADS * 2 * DIFF_HD)
EVEN_IN = sum(EVEN_SPLITS)
EVEN_MIX = MLA_HEADS * MLA_V + DIFF_HEADS * 2 * DIFF_HD
ODD_SPLITS = (GQA_HEADS * GQA_HD, GQA_KV_HEADS * GQA_HD, GQA_KV_HEADS * GQA_HD,
              NA_HEADS * NA_HD, NA_HEADS * NA_HD, NA_HEADS * NA_HD)
ODD_IN = sum(ODD_SPLITS)
ODD_MIX = GQA_HEADS * GQA_HD + NA_HEADS * NA_HD

kernel_name = 'hybrid_flow_backbone'


def rms_norm(x, g):
    xf = x.astype(jnp.float32)
    y = xf * lax.rsqrt(jnp.mean(xf * xf, axis=-1, keepdims=True) + EPS)
    return (y * g.astype(jnp.float32)).astype(x.dtype)


def modulate(u, shift, scale):
    return u * (1 + scale) + shift


def split_cols(p, sizes):
    return jnp.split(p, np.cumsum(sizes)[:-1].tolist(), axis=-1)


def axial_angles(seq_len, rot_dim):
    t = jnp.arange(seq_len, dtype=jnp.int32)
    row = (t // GRID_W).astype(jnp.float32)
    col = (t % GRID_W).astype(jnp.float32)
    n = rot_dim // 4
    inv = ROPE_THETA ** (-jnp.arange(n, dtype=jnp.float32) / n)
    return jnp.concatenate([row[:, None] * inv, col[:, None] * inv], axis=-1)


def apply_rope(x, ang):
    half = x.shape[-1] // 2
    shape = (1, ang.shape[0]) + (1,) * (x.ndim - 3) + (half,)
    cos = jnp.cos(ang).reshape(shape).astype(x.dtype)
    sin = jnp.sin(ang).reshape(shape).astype(x.dtype)
    x1, x2 = x[..., :half], x[..., half:]
    return jnp.concatenate([x1 * cos - x2 * sin, x2 * cos + x1 * sin], axis=-1)


def sweep_query_blocks(fn, *qs):
    b, s = qs[0].shape[:2]
    nb = s // Q_BLOCK
    blocks = tuple(jnp.moveaxis(q.reshape((b, nb, Q_BLOCK) + q.shape[2:]), 1, 0) for q in qs)
    out = lax.map(lambda a: fn(*a), blocks)
    out = jnp.moveaxis(out, 0, 1)
    return out.reshape((b, s) + out.shape[3:])


def attend_mla(q_nope, q_rope, k_nope, k_rope, v):
    s = (jnp.einsum('bqhd,bkhd->bhqk', q_nope, k_nope)
         + jnp.einsum('bqhr,bkr->bhqk', q_rope, k_rope)).astype(jnp.float32) * MLA_SCALE
    p = jax.nn.softmax(s, axis=-1).astype(v.dtype)
    return jnp.einsum('bhqk,bkhd->bqhd', p, v)


def attend_diff(q, k, v, lam):
    s = jnp.einsum('bqhmd,bkhmd->bhmqk', q, k).astype(jnp.float32) * DIFF_SCALE
    p = jax.nn.softmax(s, axis=-1)
    p = p[:, :, 0] - lam * p[:, :, 1]
    return jnp.einsum('bhqk,bkhv->bqhv', p.astype(v.dtype), v)


def attend_gqa(q, k, v):
    s = jnp.einsum('bqhgd,bkhd->bhgqk', q, k).astype(jnp.float32) * HEAD_SCALE
    p = jax.nn.softmax(s, axis=-1).astype(v.dtype)
    return jnp.einsum('bhgqk,bkhd->bqhgd', p, v)


def neighbourhood_attention(q, k, v, k_ctx, v_ctx, rpb):
    b, s, h, d = q.shape
    rows = s // GRID_W
    win_r = min(NA_WIN_ROWS, rows)
    n_chunks = GRID_W // NA_COL_CHUNK
    q_col = np.arange(GRID_W).reshape(n_chunks, NA_COL_CHUNK)
    band_start = np.clip(q_col[:, 0] - NA_WIN_COLS // 2, 0, GRID_W - NA_COL_BAND)
    key_col = band_start[:, None] + np.arange(NA_COL_BAND)
    win_start = np.clip(q_col - NA_WIN_COLS // 2, 0, GRID_W - NA_WIN_COLS)
    col_mask = ((key_col[:, None, :] >= win_start[..., None])
                & (key_col[:, None, :] < win_start[..., None] + NA_WIN_COLS))
    rel_col_idx = np.clip(key_col[:, None, :] - q_col[..., None] + NA_WIN_COLS - 1,
                          0, 2 * NA_WIN_COLS - 2)
    kg = k.reshape(b, rows, GRID_W, h, d)
    vg = v.reshape(b, rows, GRID_W, h, d)
    qg = q.reshape(b, rows, n_chunks, NA_COL_CHUNK, h, d)
    n_ctx = k_ctx.shape[1]

    def row_block(args):
        r, q_row = args
        r0 = jnp.clip(r - win_r // 2, 0, rows - win_r)
        k_band = lax.dynamic_slice_in_dim(kg, r0, win_r, axis=1)
        v_band = lax.dynamic_slice_in_dim(vg, r0, win_r, axis=1)
        k_sub = k_band[:, :, key_col]
        v_sub = v_band[:, :, key_col]
        s_loc = jnp.einsum('bjqhd,bijkhd->bhjqik', q_row, k_sub).astype(jnp.float32) * HEAD_SCALE
        rel_row = r0 + jnp.arange(win_r) - r + NA_WIN_ROWS - 1
        bias = rpb[:, rel_row[None, None, :, None], rel_col_idx[:, :, None, :]]
        s_loc = jnp.where(col_mask[:, :, None, :], s_loc + bias[None].astype(jnp.float32), NEG_INF)
        s_loc = s_loc.reshape(b, h, n_chunks, NA_COL_CHUNK, win_r * NA_COL_BAND)
        s_ctx = jnp.einsum('bjqhd,bkhd->bhjqk', q_row, k_ctx).astype(jnp.float32) * HEAD_SCALE
        p = jax.nn.softmax(jnp.concatenate([s_ctx, s_loc], axis=-1), axis=-1).astype(v.dtype)
        p_ctx = p[..., :n_ctx]
        p_loc = p[..., n_ctx:].reshape(b, h, n_chunks, NA_COL_CHUNK, win_r, NA_COL_BAND)
        return (jnp.einsum('bhjqk,bkhd->bjqhd', p_ctx, v_ctx)
                + jnp.einsum('bhjqik,bijkhd->bjqhd', p_loc, v_sub))

    out = lax.map(row_block, (jnp.arange(rows), jnp.moveaxis(qg, 1, 0)))
    return jnp.moveaxis(out, 0, 1).reshape(b, s, h, d)


def even_mixer(a_lat, a_ctx, w_in, w_out, mla_q_norm, mla_w_uq, mla_kv_norm, mla_w_ukv,
               diff_lambda, diff_subln, lam_init, ang_mla, ang_head, want_ctx):
    lp = diff_lambda.astype(jnp.float32)
    lam = jnp.exp(jnp.sum(lp[0] * lp[1])) - jnp.exp(jnp.sum(lp[2] * lp[3])) + lam_init

    def project(a, rope):
        b, t, _ = a.shape
        cq, ckv, kr, dq, dk, dv = split_cols(a @ w_in, EVEN_SPLITS)
        q = (rms_norm(cq, mla_q_norm) @ mla_w_uq).reshape(b, t, MLA_HEADS, MLA_NOPE + MLA_ROPE)
        kv = (rms_norm(ckv, mla_kv_norm) @ mla_w_ukv).reshape(b, t, MLA_HEADS, MLA_NOPE + MLA_V)
        qn, qr = q[..., :MLA_NOPE], q[..., MLA_NOPE:]
        kn, v = kv[..., :MLA_NOPE], kv[..., MLA_NOPE:]
        dq = dq.reshape(b, t, DIFF_HEADS, 2, DIFF_HD)
        dk = dk.reshape(b, t, DIFF_HEADS, 2, DIFF_HD)
        dv = dv.reshape(b, t, DIFF_HEADS, 2 * DIFF_HD)
        if rope:
            qr, kr = apply_rope(qr, ang_mla), apply_rope(kr, ang_mla)
            dq, dk = apply_rope(dq, ang_head), apply_rope(dk, ang_head)
        return qn, qr, kn, kr, v, dq, dk, dv

    def merge(o_mla, o_diff):
        b, t = o_mla.shape[:2]
        o_diff = rms_norm(o_diff, diff_subln) * (1.0 - lam_init)
        return jnp.concatenate([o_mla.reshape(b, t, -1), o_diff.reshape(b, t, -1)], axis=-1) @ w_out

    qn_c, qr_c, kn_c, kr_c, v_c, dq_c, dk_c, dv_c = project(a_ctx, False)
    qn, qr, kn, kr, v, dq, dk, dv = project(a_lat, True)
    cat = lambda u, w: jnp.concatenate([u, w], axis=1)
    kn_all, kr_all, v_all = cat(kn_c, kn), cat(kr_c, kr), cat(v_c, v)
    dk_all, dv_all = cat(dk_c, dk), cat(dv_c, dv)
    o_lat = merge(sweep_query_blocks(lambda a, b: attend_mla(a, b, kn_all, kr_all, v_all), qn, qr),
                  sweep_query_blocks(lambda a: attend_diff(a, dk_all, dv_all, lam), dq))
    o_ctx = None
    if want_ctx:
        o_ctx = merge(attend_mla(qn_c, qr_c, kn_c, kr_c, v_c), attend_diff(dq_c, dk_c, dv_c, lam))
    return o_lat, o_ctx


def odd_mixer(a_lat, a_ctx, w_in, w_out, gqa_qk_norm, na_rpb, ang_head, want_ctx):
    def project(a, rope):
        b, t, _ = a.shape
        gq, gk, gv, nq, nk, nv = split_cols(a @ w_in, ODD_SPLITS)
        gq = rms_norm(gq.reshape(b, t, GQA_KV_HEADS, GQA_GROUP, GQA_HD), gqa_qk_norm[0])
        gk = rms_norm(gk.reshape(b, t, GQA_KV_HEADS, GQA_HD), gqa_qk_norm[1])
        gv = gv.reshape(b, t, GQA_KV_HEADS, GQA_HD)
        nq = nq.reshape(b, t, NA_HEADS, NA_HD)
        nk = nk.reshape(b, t, NA_HEADS, NA_HD)
        nv = nv.reshape(b, t, NA_HEADS, NA_HD)
        if rope:
            gq, gk = apply_rope(gq, ang_head), apply_rope(gk, ang_head)
        return gq, gk, gv, nq, nk, nv

    def merge(o_gqa, o_na):
        b, t = o_gqa.shape[:2]
        return jnp.concatenate([o_gqa.reshape(b, t, -1), o_na.reshape(b, t, -1)], axis=-1) @ w_out

    gq_c, gk_c, gv_c, nq_c, nk_c, nv_c = project(a_ctx, False)
    gq, gk, gv, nq, nk, nv = project(a_lat, True)
    gk_all = jnp.concatenate([gk_c, gk], axis=1)
    gv_all = jnp.concatenate([gv_c, gv], axis=1)
    o_lat = merge(sweep_query_blocks(lambda a: attend_gqa(a, gk_all, gv_all), gq),
                  neighbourhood_attention(nq, nk, nv, nk_c, nv_c, na_rpb))
    o_ctx = None
    if want_ctx:
        o_ctx = merge(attend_gqa(gq_c, gk_c, gv_c), attend_gqa(nq_c[:, :, :, None], nk_c, nv_c))
    return o_lat, o_ctx


def channel_mixer(a, w1, w2):
    return jnp.square(jax.nn.relu(a @ w1)) @ w2


def setup_inputs(seed: int = 0) -> dict:
    key = jax.random.key(seed)
    ks = jax.random.split(key, 22)
    n_even, n_odd = (DEPTH + 1) // 2, DEPTH // 2
    f32 = jnp.float32
    nrm = lambda k, shape: jax.random.normal(k, shape, f32)
    lin = lambda k, shape, fan_in: nrm(k, shape) * fan_in ** -0.5
    gain = lambda k, shape: 1.0 + 0.02 * nrm(k, shape)
    return {
        'x': nrm(ks[0], (BATCH, SEQ, D_MODEL)),
        'c': nrm(ks[1], (BATCH, D_MODEL)),
        'ctx': nrm(ks[2], (BATCH, CTX_LEN, D_MODEL)),
        'c_ctx': nrm(ks[3], (D_MODEL,)),
        'w_ada': lin(ks[4], (DEPTH, D_MODEL, 6 * D_MODEL), D_MODEL),
        'b_ada': 0.01 * nrm(ks[5], (DEPTH, 6 * D_MODEL)),
        'norm_g': gain(ks[6], (DEPTH, 4, D_MODEL)),
        'w_ff1': lin(ks[7], (DEPTH, D_MODEL, D_FF), D_MODEL),
        'w_ff2': lin(ks[8], (DEPTH, D_FF, D_MODEL), D_FF),
        'w_in_even': lin(ks[9], (n_even, D_MODEL, EVEN_IN), D_MODEL),
        'w_out_even': lin(ks[10], (n_even, EVEN_MIX, D_MODEL), EVEN_MIX),
        'mla_q_norm': gain(ks[11], (n_even, MLA_Q_RANK)),
        'mla_w_uq': lin(ks[12], (n_even, MLA_Q_RANK, MLA_HEADS * (MLA_NOPE + MLA_ROPE)), MLA_Q_RANK),
        'mla_kv_norm': gain(ks[13], (n_even, MLA_KV_RANK)),
        'mla_w_ukv': lin(ks[14], (n_even, MLA_KV_RANK, MLA_HEADS * (MLA_NOPE + MLA_V)), MLA_KV_RANK),
        'diff_lambda': 0.1 * nrm(ks[15], (n_even, 4, DIFF_HD)),
        'diff_subln': gain(ks[16], (n_even, 2 * DIFF_HD)),
        'w_in_odd': lin(ks[17], (n_odd, D_MODEL, ODD_IN), D_MODEL),
        'w_out_odd': lin(ks[18], (n_odd, ODD_MIX, D_MODEL), ODD_MIX),
        'gqa_qk_norm': gain(ks[19], (n_odd, 2, GQA_HD)),
        'na_rpb': 0.1 * nrm(ks[20], (n_odd, NA_HEADS, 2 * NA_WIN_ROWS - 1, 2 * NA_WIN_COLS - 1)),
    }


def reference(x, c, ctx, c_ctx, w_ada, b_ada, norm_g, w_ff1, w_ff2,
              w_in_even, w_out_even, mla_q_norm, mla_w_uq, mla_kv_norm, mla_w_ukv,
              diff_lambda, diff_subln, w_in_odd, w_out_odd, gqa_qk_norm, na_rpb):
    seq = x.shape[1]
    ang_mla = axial_angles(seq, MLA_ROPE)
    ang_head = axial_angles(seq, GQA_HD)
    silu_c, silu_cc = jax.nn.silu(c), jax.nn.silu(c_ctx)
    h, hc = x, ctx
    for layer in range(DEPTH):
        want_ctx = layer < DEPTH - 1
        mod = jnp.split((silu_c @ w_ada[layer] + b_ada[layer])[:, None, :], 6, axis=-1)
        mod_c = jnp.split(silu_cc @ w_ada[layer] + b_ada[layer], 6, axis=-1)
        g = norm_g[layer]
        a_lat = modulate(rms_norm(h, g[0]), mod[0], mod[1])
        a_ctx = modulate(rms_norm(hc, g[0]), mod_c[0], mod_c[1])
        i = layer // 2
        if layer % 2 == 0:
            lam_init = 0.8 - 0.6 * math.exp(-0.3 * layer)
            o_lat, o_ctx = even_mixer(a_lat, a_ctx, w_in_even[i], w_out_even[i], mla_q_norm[i], mla_w_uq[i],
                                      mla_kv_norm[i], mla_w_ukv[i], diff_lambda[i], diff_subln[i],
                                      lam_init, ang_mla, ang_head, want_ctx)
        else:
            o_lat, o_ctx = odd_mixer(a_lat, a_ctx, w_in_odd[i], w_out_odd[i], gqa_qk_norm[i], na_rpb[i],
                                     ang_head, want_ctx)
        h = h + mod[2] * rms_norm(o_lat, g[1])
        f = channel_mixer(modulate(rms_norm(h, g[2]), mod[3], mod[4]), w_ff1[layer], w_ff2[layer])
        h = h + mod[5] * rms_norm(f, g[3])
        if want_ctx:
            hc = hc + mod_c[2] * rms_norm(o_ctx, g[1])
            fc = channel_mixer(modulate(rms_norm(hc, g[2]), mod_c[3], mod_c[4]), w_ff1[layer], w_ff2[layer])
            hc = hc + mod_c[5] * rms_norm(fc, g[3])
    return h
```

```python
import functools
import math

import numpy as np
import jax
import jax.numpy as jnp
from jax import lax
from jax.experimental import pallas as pl
from jax.experimental.pallas import tpu as pltpu

F32 = jnp.float32
BF16 = jnp.bfloat16

D_MODEL = 1024
DEPTH = 4
GRID_W = 64
ROPE_THETA = 10000.0
EPS = 1e-6
NEG_INF = -1e30
D_FF = 4 * D_MODEL
LANES = 128
HALF = 64

MLA_HEADS = 8
MLA_NOPE = 64
MLA_ROPE = 32
MLA_Q_RANK = 256
MLA_KV_RANK = 128
MLA_SCALE = (MLA_NOPE + MLA_ROPE) ** -0.5
DIFF_HEADS = 4
DIFF_HD = 64
DIFF_SCALE = DIFF_HD ** -0.5
GQA_HD = 64
HEAD_SCALE = GQA_HD ** -0.5
NA_WIN_ROWS = 8
NA_WIN_COLS = 16
NA_Q_ROWS = 4
NA_SPAN_ROWS = 12
NA_TQ = NA_Q_ROWS * GRID_W
NA_SPAN = NA_SPAN_ROWS * GRID_W

VMEM_LIMIT_MB = 56


def _cparams(n_grid, vmem_mb=VMEM_LIMIT_MB):
    return pltpu.CompilerParams(dimension_semantics=("arbitrary",) * n_grid,
                                vmem_limit_bytes=vmem_mb << 20)


def _full(shape):
    nd = len(shape)
    return pl.BlockSpec(shape, lambda *_: (0,) * nd)


def _rms(x, g):
    return x * lax.rsqrt(jnp.mean(x * x, axis=-1, keepdims=True) + EPS) * g


def _dot(a, b):
    return jnp.dot(a, b, preferred_element_type=F32)


def _dot_nt(a, b):
    return lax.dot_general(a, b, (((1,), (1,)), ((), ())), preferred_element_type=F32)


def _lane_lo(rows):
    return lax.broadcasted_iota(jnp.int32, (rows, LANES), 1) < HALF


def _mod_kernel(cc_ref, w_ref, b_ref, o_ref):
    x = cc_ref[...]
    s = x / (1.0 + jnp.exp(-x))
    o_ref[0] = _dot(s.astype(BF16), w_ref[0].astype(BF16)) + b_ref[0]


def _modulation(cc, w_ada, b_ada):
    rows = cc.shape[0]
    tn = 1536
    return pl.pallas_call(
        _mod_kernel,
        grid=(DEPTH, 6 * D_MODEL // tn),
        in_specs=[_full((rows, D_MODEL)),
                  pl.BlockSpec((1, D_MODEL, tn), lambda l, n: (l, 0, n)),
                  pl.BlockSpec((1, 1, tn), lambda l, n: (l, 0, n))],
        out_specs=pl.BlockSpec((1, rows, tn), lambda l, n: (l, 0, n)),
        out_shape=jax.ShapeDtypeStruct((DEPTH, rows, 6 * D_MODEL), F32),
        compiler_params=_cparams(2),
        name="adaln_mod",
    )(cc, w_ada, b_ada.reshape(DEPTH, 1, 6 * D_MODEL))


def _modulated_input(h_ref, mod_ref, g_ref):
    mod = mod_ref[0]
    a = _rms(h_ref[0], g_ref[...]) * (1.0 + mod[:, D_MODEL:2 * D_MODEL]) + mod[:, 0:D_MODEL]
    return a.astype(BF16)


def _proj_even_kernel(*refs, rope):
    h_ref, mod_ref, g_ref, win_ref, qn_ref, wuq_ref, kvn_ref, wukv_ref = refs[:8]
    qm_ref, km_ref, vm_ref, dq_ref, dk_ref, dv_ref = refs[-6:]
    y = _dot(_modulated_input(h_ref, mod_ref, g_ref), win_ref[...])
    q = _dot(_rms(y[:, 0:256], qn_ref[...]).astype(BF16), wuq_ref[...])
    kv = _dot(_rms(y[:, 256:384], kvn_ref[...]).astype(BF16), wukv_ref[...])
    krb = y[:, 384:512]
    if rope:
        cm, sma, smb, ch, sha, shb = (r[...] for r in refs[8:14])
        rope_m = lambda t: t * cm + pltpu.roll(t, 112, 1) * sma + pltpu.roll(t, 16, 1) * smb
        rope_h = lambda t: t * ch + pltpu.roll(t, 96, 1) * sha + pltpu.roll(t, 32, 1) * shb
    else:
        rope_m = rope_h = lambda t: t
    krb = rope_m(krb)
    for h in range(MLA_HEADS):
        sl = slice(h * LANES, (h + 1) * LANES)
        qm_ref[0, h] = (rope_m(q[:, sl]) * MLA_SCALE).astype(BF16)
        km_ref[0, h] = (kv[:, sl] + krb).astype(BF16)
    for p in range(4):
        sl = slice(p * LANES, (p + 1) * LANES)
        vm_ref[0, p] = kv[:, 1024 + p * LANES:1024 + (p + 1) * LANES].astype(BF16)
        dq_ref[0, p] = (rope_h(y[:, 512:1024][:, sl]) * DIFF_SCALE).astype(BF16)
        dk_ref[0, p] = rope_h(y[:, 1024:1536][:, sl]).astype(BF16)
        dv_ref[0, p] = y[:, 1536:2048][:, sl].astype(BF16)


def _proj_odd_kernel(*refs, rope):
    h_ref, mod_ref, g_ref, win_ref, gqn_ref, gkn_ref = refs[:6]
    gq_ref, gkk_ref, gvv_ref, nq_ref, nk_ref, nv_ref = refs[-6:]
    y = _dot(_modulated_input(h_ref, mod_ref, g_ref), win_ref[...])
    tm = y.shape[0]
    lo = _lane_lo(tm)
    if rope:
        ch, sha, shb = (r[...] for r in refs[6:9])
        rope_h = lambda t: t * ch + pltpu.roll(t, 96, 1) * sha + pltpu.roll(t, 32, 1) * shb
    else:
        rope_h = lambda t: t

    def head_norm(t, g):
        sq = t * t
        ms_lo = jnp.sum(jnp.where(lo, sq, 0.0), axis=-1, keepdims=True) * (1.0 / HALF)
        ms_hi = jnp.sum(jnp.where(lo, 0.0, sq), axis=-1, keepdims=True) * (1.0 / HALF)
        r = jnp.where(lo, lax.rsqrt(ms_lo + EPS), lax.rsqrt(ms_hi + EPS))
        return t * r * g

    def dup_halves(t):
        sw = pltpu.roll(t, HALF, 1)
        return jnp.where(lo, t, sw), jnp.where(lo, sw, t)

    gqn = gqn_ref[...]
    for p in range(4):
        sl = slice(p * LANES, (p + 1) * LANES)
        gq_ref[0, p] = (rope_h(head_norm(y[:, 0:512][:, sl], gqn)) * HEAD_SCALE).astype(BF16)
        nq_ref[0, p] = (y[:, 768:1280][:, sl] * HEAD_SCALE).astype(BF16)
        nk_ref[0, p] = y[:, 1280:1792][:, sl].astype(BF16)
        nv_ref[0, p] = y[:, 1792:2304][:, sl].astype(BF16)
    k0, k1 = dup_halves(rope_h(head_norm(y[:, 512:640], gkn_ref[...])))
    v0, v1 = dup_halves(y[:, 640:768])
    gkk_ref[0, 0] = k0.astype(BF16)
    gkk_ref[0, 1] = k1.astype(BF16)
    gvv_ref[0, 0] = v0.astype(BF16)
    gvv_ref[0, 1] = v1.astype(BF16)


def _project(kernel, h, mod, mod_row, g0, weights, tables, out_heads, tm, name):
    b, t, _ = h.shape
    grid = (t // tm, b)
    mod_map = (lambda s, i: (i, 0, 0)) if mod_row is None else (lambda s, i: (mod_row, 0, 0))
    in_specs = [pl.BlockSpec((1, tm, D_MODEL), lambda s, i: (i, s, 0)),
                pl.BlockSpec((1, 1, 6 * D_MODEL), mod_map),
                _full(g0.shape)]
    in_specs += [_full(w.shape) for w in weights]
    in_specs += [pl.BlockSpec((tm, LANES), lambda s, i: (s, 0)) for _ in tables]
    out_specs = [pl.BlockSpec((1, nh, tm, LANES), lambda s, i: (i, 0, s, 0)) for nh in out_heads]
    out_shape = [jax.ShapeDtypeStruct((b, nh, t, LANES), BF16) for nh in out_heads]
    return pl.pallas_call(
        functools.partial(kernel, rope=bool(tables)),
        grid=grid, in_specs=in_specs, out_specs=out_specs, out_shape=out_shape,
        compiler_params=_cparams(2), name=name,
    )(h, mod, g0, *weights, *tables)


def _softmax_parts(scores):
    mx = functools.reduce(jnp.maximum, [s.max(axis=-1, keepdims=True) for s in scores])
    ps = [jnp.exp(s - mx) for s in scores]
    den = functools.reduce(jnp.add, [p.sum(axis=-1, keepdims=True) for p in ps])
    return ps, den


def _attend(q, ks, vs):
    ps, den = _softmax_parts([_dot_nt(q, k) for k in ks])
    o = functools.reduce(jnp.add, [_dot(p.astype(BF16), v) for p, v in zip(ps, vs)])
    return o / den


def _split_halves(q):
    lo = _lane_lo(q.shape[0])
    qf = q.astype(F32)
    return jnp.where(lo, qf, 0.0).astype(BF16), jnp.where(lo, 0.0, qf).astype(BF16)


def _pair_kernel(*refs, n_seg, split):
    o_ref = refs[-1]
    if split:
        qa, qb = _split_halves(refs[0][0, 0])
        ka = kb = [r[0, 0] for r in refs[1:1 + n_seg]]
        vs = [r[0, 0] for r in refs[1 + n_seg:1 + 2 * n_seg]]
    else:
        qa, qb = refs[0][0, 0], refs[1][0, 0]
        ka = [r[0, 0] for r in refs[2:2 + n_seg]]
        kb = [r[0, 0] for r in refs[2 + n_seg:2 + 2 * n_seg]]
        vs = [r[0, 0] for r in refs[2 + 2 * n_seg:2 + 3 * n_seg]]
    oa = _attend(qa, ka, vs)
    ob = _attend(qb, kb, vs)
    o_ref[0] = jnp.where(_lane_lo(oa.shape[0]), oa, ob).astype(o_ref.dtype)


def _diff_kernel(*refs, n_seg, lam_init):
    q_ref, lam_ref, subln_ref = refs[0], refs[1], refs[2]
    ks = [r[0, 0] for r in refs[3:3 + n_seg]]
    vs = [r[0, 0] for r in refs[3 + n_seg:3 + 2 * n_seg]]
    o_ref = refs[-1]
    lp = lam_ref[...]
    lam = (jnp.exp(jnp.sum(lp[0:1] * lp[1:2], axis=-1, keepdims=True))
           - jnp.exp(jnp.sum(lp[2:3] * lp[3:4], axis=-1, keepdims=True)) + lam_init)
    q1, q2 = _split_halves(q_ref[0, 0])
    p1, den1 = _softmax_parts([_dot_nt(q1, k) for k in ks])
    p2, den2 = _softmax_parts([_dot_nt(q2, k) for k in ks])
    w1 = 1.0 / den1
    w2 = lam / den2
    o = functools.reduce(jnp.add, [_dot((a * w1 - w2 * b).astype(BF16), v)
                                   for a, b, v in zip(p1, p2, vs)])
    o_ref[0] = (_rms(o, subln_ref[...]) * (1.0 - lam_init)).astype(o_ref.dtype)


def _gqa_kernel(*refs, n_seg):
    q_ref, o_ref = refs[0], refs[-1]
    ks = [r[0, 0] for r in refs[1:1 + n_seg]]
    vs = [r[0, 0] for r in refs[1 + n_seg:1 + 2 * n_seg]]
    tq = q_ref.shape[2]
    q = jnp.concatenate(_split_halves(q_ref[0, 0]) + _split_halves(q_ref[0, 1]), axis=0)
    o = _attend(q, ks, vs)
    lo = _lane_lo(tq)
    o_ref[0, :, 0:LANES] = jnp.where(lo, o[0:tq], o[tq:2 * tq]).astype(o_ref.dtype)
    o_ref[0, :, LANES:2 * LANES] = jnp.where(lo, o[2 * tq:3 * tq], o[3 * tq:4 * tq]).astype(o_ref.dtype)


def _na_kernel(q_ref, kc_ref, vc_ref, kl_ref, vl_ref, bias_ref, o_ref):
    j = pl.program_id(1)
    start = pl.multiple_of(jnp.clip(NA_Q_ROWS * j - NA_WIN_ROWS // 2, 0, 32 - NA_SPAN_ROWS) * GRID_W, GRID_W)
    lo = _lane_lo(NA_TQ)
    for p in range(4):
        kloc = kl_ref[0, p, pl.ds(start, NA_SPAN), :]
        vloc = vl_ref[0, p, pl.ds(start, NA_SPAN), :]
        kc, vc = kc_ref[0, p], vc_ref[0, p]
        outs = []
        for half, qh in enumerate(_split_halves(q_ref[0, p])):
            s_ctx = _dot_nt(qh, kc)
            s_loc = _dot_nt(qh, kloc) + bias_ref[0, 2 * p + half]
            (p_ctx, p_loc), den = _softmax_parts([s_ctx, s_loc])
            outs.append((_dot(p_ctx.astype(BF16), vc) + _dot(p_loc.astype(BF16), vloc)) / den)
        o_ref[0, :, p * LANES:(p + 1) * LANES] = jnp.where(lo, outs[0], outs[1]).astype(o_ref.dtype)


def _seg_specs(arrs, head_map):
    return [pl.BlockSpec((1, 1, a.shape[2], LANES), head_map) for a in arrs]


def _mla_attention(qm, kms, vms, tq):
    b, _, t, _ = qm.shape
    n_seg = len(kms)
    in_specs = ([pl.BlockSpec((1, 1, tq, LANES), lambda i, p, s: (i, 2 * p, s, 0)),
                 pl.BlockSpec((1, 1, tq, LANES), lambda i, p, s: (i, 2 * p + 1, s, 0))]
                + _seg_specs(kms, lambda i, p, s: (i, 2 * p, 0, 0))
                + _seg_specs(kms, lambda i, p, s: (i, 2 * p + 1, 0, 0))
                + _seg_specs(vms, lambda i, p, s: (i, p, 0, 0)))
    return pl.pallas_call(
        functools.partial(_pair_kernel, n_seg=n_seg, split=False),
        grid=(b, 4, t // tq), in_specs=in_specs,
        out_specs=pl.BlockSpec((1, tq, LANES), lambda i, p, s: (i, s, p)),
        out_shape=jax.ShapeDtypeStruct((b, t, 4 * LANES), BF16),
        compiler_params=_cparams(3), name="mla_attn",
    )(qm, qm, *kms, *kms, *vms)


def _split_pair_attention(q, ks, vs, tq, name):
    b, _, t, _ = q.shape
    n_seg = len(ks)
    in_specs = ([pl.BlockSpec((1, 1, tq, LANES), lambda i, p, s: (i, p, s, 0))]
                + _seg_specs(ks, lambda i, p, s: (i, p, 0, 0))
                + _seg_specs(vs, lambda i, p, s: (i, p, 0, 0)))
    return pl.pallas_call(
        functools.partial(_pair_kernel, n_seg=n_seg, split=True),
        grid=(b, 4, t // tq), in_specs=in_specs,
        out_specs=pl.BlockSpec((1, tq, LANES), lambda i, p, s: (i, s, p)),
        out_shape=jax.ShapeDtypeStruct((b, t, 4 * LANES), BF16),
        compiler_params=_cparams(3), name=name,
    )(q, *ks, *vs)


def _diff_attention(dq, dks, dvs, diff_lambda, subln, lam_init, tq):
    b, _, t, _ = dq.shape
    n_seg = len(dks)
    in_specs = ([pl.BlockSpec((1, 1, tq, LANES), lambda i, h, s: (i, h, s, 0)),
                 _full(diff_lambda.shape), _full(subln.shape)]
                + _seg_specs(dks, lambda i, h, s: (i, h, 0, 0))
                + _seg_specs(dvs, lambda i, h, s: (i, h, 0, 0)))
    return pl.pallas_call(
        functools.partial(_diff_kernel, n_seg=n_seg, lam_init=lam_init),
        grid=(b, DIFF_HEADS, t // tq), in_specs=in_specs,
        out_specs=pl.BlockSpec((1, tq, LANES), lambda i, h, s: (i, s, h)),
        out_shape=jax.ShapeDtypeStruct((b, t, DIFF_HEADS * LANES), BF16),
        compiler_params=_cparams(3), name="diff_attn",
    )(dq, diff_lambda, subln, *dks, *dvs)


def _gqa_attention(gq, kks, vvs, tq):
    b, _, t, _ = gq.shape
    n_seg = len(kks)
    in_specs = ([pl.BlockSpec((1, 2, tq, LANES), lambda i, g, s: (i, g, s, 0))]
                + _seg_specs(kks, lambda i, g, s: (i, g, 0, 0))
                + _seg_specs(vvs, lambda i, g, s: (i, g, 0, 0)))
    return pl.pallas_call(
        functools.partial(_gqa_kernel, n_seg=n_seg),
        grid=(b, 2, t // tq), in_specs=in_specs,
        out_specs=pl.BlockSpec((1, tq, 2 * LANES), lambda i, g, s: (i, s, g)),
        out_shape=jax.ShapeDtypeStruct((b, t, 4 * LANES), BF16),
        compiler_params=_cparams(3), name="gqa_attn",
    )(gq, *kks, *vvs)


def _na_bias_tables(rpb):
    n_steps = 32 // NA_Q_ROWS
    js = np.array([0, 1, n_steps - 1])
    a, qc, i, kc = np.arange(NA_Q_ROWS), np.arange(GRID_W), np.arange(NA_SPAN_ROWS), np.arange(GRID_W)
    r = NA_Q_ROWS * js[:, None] + a[None, :]
    start = np.clip(NA_Q_ROWS * js - NA_WIN_ROWS // 2, 0, 32 - NA_SPAN_ROWS)
    kr = start[:, None] + i[None, :]
    r0 = np.clip(r - NA_WIN_ROWS // 2, 0, 32 - NA_WIN_ROWS)
    valid_row = (kr[:, None, :] >= r0[:, :, None]) & (kr[:, None, :] < r0[:, :, None] + NA_WIN_ROWS)
    rel_row = np.clip(kr[:, None, :] - r[:, :, None] + NA_WIN_ROWS - 1, 0, 2 * NA_WIN_ROWS - 2)
    ws = np.clip(qc - NA_WIN_COLS // 2, 0, GRID_W - NA_WIN_COLS)
    valid_col = (kc[None, :] >= ws[:, None]) & (kc[None, :] < ws[:, None] + NA_WIN_COLS)
    rel_col = np.clip(kc[None, :] - qc[:, None] + NA_WIN_COLS - 1, 0, 2 * NA_WIN_COLS - 2)
    shape = (3, NA_Q_ROWS, GRID_W, NA_SPAN_ROWS, GRID_W)
    rr = np.broadcast_to(rel_row[:, :, None, :, None], shape)
    rc = np.broadcast_to(rel_col[None, None, :, None, :], shape)
    valid = np.broadcast_to(valid_row[:, :, None, :, None] & valid_col[None, None, :, None, :], shape)
    vals = rpb[:, rr, rc]
    tbl = jnp.where(valid[None], vals, NEG_INF)
    return jnp.transpose(tbl, (1, 0, 2, 3, 4, 5)).reshape(3, 8, NA_TQ, NA_SPAN)


def _na_attention(nq, nk_c, nv_c, nk, nv, bias):
    b, _, t, _ = nq.shape
    n_steps = t // NA_TQ
    ctx_len = nk_c.shape[2]
    pair = lambda i, j: (i, 0, 0, 0)
    return pl.pallas_call(
        _na_kernel,
        grid=(b, n_steps),
        in_specs=[pl.BlockSpec((1, 4, NA_TQ, LANES), lambda i, j: (i, 0, j, 0)),
                  pl.BlockSpec((1, 4, ctx_len, LANES), pair),
                  pl.BlockSpec((1, 4, ctx_len, LANES), pair),
                  pl.BlockSpec((1, 4, t, LANES), pair),
                  pl.BlockSpec((1, 4, t, LANES), pair),
                  pl.BlockSpec((1, 8, NA_TQ, NA_SPAN),
                               lambda i, j: (jnp.minimum(j, 1) + jnp.maximum(j - (n_steps - 2), 0), 0, 0, 0))],
        out_specs=pl.BlockSpec((1, NA_TQ, 4 * LANES), lambda i, j: (i, j, 0)),
        out_shape=jax.ShapeDtypeStruct((b, t, 4 * LANES), BF16),
        compiler_params=_cparams(2), name="na_attn",
    )(nq, nk_c, nv_c, nk, nv, bias)


def _post_kernel(h_ref, ma_ref, mb_ref, mod_ref, g_ref, wout_ref, w1_ref, w2_ref, o_ref):
    mod = mod_ref[0]
    g = g_ref[...]
    half = ma_ref.shape[2]
    o = _dot(ma_ref[0], wout_ref[0:half, :]) + _dot(mb_ref[0], wout_ref[half:2 * half, :])
    h1 = h_ref[0] + mod[:, 2 * D_MODEL:3 * D_MODEL] * _rms(o, g[1:2])
    a2 = (_rms(h1, g[2:3]) * (1.0 + mod[:, 4 * D_MODEL:5 * D_MODEL]) + mod[:, 3 * D_MODEL:4 * D_MODEL]).astype(BF16)
    f = None
    ck = 1024
    for c in range(D_FF // ck):
        u = jnp.square(jnp.maximum(_dot(a2, w1_ref[:, c * ck:(c + 1) * ck]), 0.0)).astype(BF16)
        part = _dot(u, w2_ref[c * ck:(c + 1) * ck, :])
        f = part if f is None else f + part
    o_ref[0] = h1 + mod[:, 5 * D_MODEL:6 * D_MODEL] * _rms(f, g[3:4])


def _post(h, mix_a, mix_b, mod, mod_row, g, w_out, w1, w2, tm):
    b, t, _ = h.shape
    mod_map = (lambda i, s: (i, 0, 0)) if mod_row is None else (lambda i, s: (mod_row, 0, 0))
    tok = lambda w: pl.BlockSpec((1, tm, w), lambda i, s: (i, s, 0))
    return pl.pallas_call(
        _post_kernel,
        grid=(b, t // tm),
        in_specs=[tok(D_MODEL), tok(mix_a.shape[2]), tok(mix_b.shape[2]),
                  pl.BlockSpec((1, 1, 6 * D_MODEL), mod_map),
                  _full(g.shape), _full(w_out.shape), _full(w1.shape), _full(w2.shape)],
        out_specs=tok(D_MODEL),
        out_shape=jax.ShapeDtypeStruct(h.shape, F32),
        compiler_params=_cparams(2), name="post_mlp",
    )(h, mix_a, mix_b, mod, g, w_out, w1, w2)


def _axial_angles(seq_len, rot_dim):
    t = jnp.arange(seq_len, dtype=jnp.int32)
    row = (t // GRID_W).astype(F32)
    col = (t % GRID_W).astype(F32)
    n = rot_dim // 4
    inv = ROPE_THETA ** (-jnp.arange(n, dtype=F32) / n)
    return jnp.concatenate([row[:, None] * inv, col[:, None] * inv], axis=-1)


def _rope_tables(seq_len):
    z = lambda n: jnp.zeros((seq_len, n), F32)
    one = lambda n: jnp.ones((seq_len, n), F32)
    am = _axial_angles(seq_len, MLA_ROPE)
    cm, sm = jnp.cos(am), jnp.sin(am)
    mla = (jnp.concatenate([one(64), cm, cm, one(32)], axis=1),
           jnp.concatenate([z(64), -sm, z(48)], axis=1),
           jnp.concatenate([z(80), sm, z(32)], axis=1))
    ah = _axial_angles(seq_len, GQA_HD)
    chh, shh = jnp.cos(ah), jnp.sin(ah)
    head = (jnp.concatenate([chh] * 4, axis=1),
            jnp.concatenate([-shh, z(32), -shh, z(32)], axis=1),
            jnp.concatenate([z(32), shh, z(32), shh], axis=1))
    return mla, head


def _even_weights(w_in, w_uq, w_ukv):
    z = lambda n: jnp.zeros((D_MODEL, n), F32)
    w_in_r = jnp.concatenate([w_in[:, 0:384], z(64), w_in[:, 384:416], z(32), w_in[:, 416:]], axis=1)
    w_uq_r = jnp.pad(w_uq.reshape(MLA_Q_RANK, MLA_HEADS, MLA_NOPE + MLA_ROPE),
                     ((0, 0), (0, 0), (0, LANES - MLA_NOPE - MLA_ROPE))).reshape(MLA_Q_RANK, MLA_HEADS * LANES)
    kv = w_ukv.reshape(MLA_KV_RANK, MLA_HEADS, 2 * HALF)
    kn = jnp.pad(kv[:, :, :HALF], ((0, 0), (0, 0), (0, HALF))).reshape(MLA_KV_RANK, MLA_HEADS * LANES)
    v = kv[:, :, HALF:].reshape(MLA_KV_RANK, MLA_HEADS * HALF)
    return w_in_r.astype(BF16), w_uq_r.astype(BF16), jnp.concatenate([kn, v], axis=1).astype(BF16)


def kernel(x, c, ctx, c_ctx, w_ada, b_ada, norm_g, w_ff1, w_ff2, w_in_even, w_out_even, mla_q_norm, mla_w_uq,
           mla_kv_norm, mla_w_ukv, diff_lambda, diff_subln, w_in_odd, w_out_odd, gqa_qk_norm, na_rpb):
    b, seq, _ = x.shape
    ctx_len = ctx.shape[1]
    assert seq == 32 * GRID_W and seq % NA_TQ == 0 and ctx_len % 256 == 0
    ctx_row = b
    cc = jnp.concatenate([c, c_ctx[None], jnp.zeros((7, D_MODEL), F32)], axis=0)
    mod_all = _modulation(cc, w_ada, b_ada)
    rope_mla, rope_head = _rope_tables(seq)
    h, hc = x, ctx
    for layer in range(DEPTH):
        want_ctx = layer < DEPTH - 1
        mod = mod_all[layer].reshape(b + 8, 1, 6 * D_MODEL)
        g = norm_g[layer]
        g0 = g[0:1]
        i = layer // 2
        if layer % 2 == 0:
            lam_init = 0.8 - 0.6 * math.exp(-0.3 * layer)
            weights = (*_even_weights(w_in_even[i], mla_w_uq[i], mla_w_ukv[i]),)
            weights = (weights[0], mla_q_norm[i][None], weights[1], mla_kv_norm[i][None], weights[2])
            heads = (8, 8, 4, 4, 4, 4)
            qm, km, vm, dq, dk, dv = _project(_proj_even_kernel, h, mod, None, g0, weights,
                                              rope_mla + rope_head, heads, 512, "proj_even")
            qm_c, km_c, vm_c, dq_c, dk_c, dv_c = _project(_proj_even_kernel, hc, mod, ctx_row, g0, weights,
                                                          (), heads, ctx_len, "proj_even_ctx")
            subln = diff_subln[i][None]
            mix_a = _mla_attention(qm, (km_c, km), (vm_c, vm), 256)
            mix_b = _diff_attention(dq, (dk_c, dk), (dv_c, dv), diff_lambda[i], subln, lam_init, 256)
            if want_ctx:
                mix_a_c = _mla_attention(qm_c, (km_c,), (vm_c,), ctx_len)
                mix_b_c = _diff_attention(dq_c, (dk_c,), (dv_c,), diff_lambda[i], subln, lam_init, ctx_len)
            w_out = w_out_even[i].astype(BF16)
        else:
            qkn = gqa_qk_norm[i]
            weights = (w_in_odd[i].astype(BF16), jnp.tile(qkn[0], 2)[None], jnp.tile(qkn[1], 2)[None])
            heads = (4, 2, 2, 4, 4, 4)
            gq, gkk, gvv, nq, nk, nv = _project(_proj_odd_kernel, h, mod, None, g0, weights,
                                                rope_head, heads, 512, "proj_odd")
            gq_c, gkk_c, gvv_c, nq_c, nk_c, nv_c = _project(_proj_odd_kernel, hc, mod, ctx_row, g0, weights,
                                                            (), heads, ctx_len, "proj_odd_ctx")
            mix_a = _gqa_attention(gq, (gkk_c, gkk), (gvv_c, gvv), 128)
            mix_b = _na_attention(nq, nk_c, nv_c, nk, nv, _na_bias_tables(na_rpb[i]))
            if want_ctx:
                mix_a_c = _gqa_attention(gq_c, (gkk_c,), (gvv_c,), ctx_len)
                mix_b_c = _split_pair_attention(nq_c, (nk_c,), (nv_c,), ctx_len, "na_ctx_attn")
            w_out = w_out_odd[i].astype(BF16)
        w1, w2 = w_ff1[layer].astype(BF16), w_ff2[layer].astype(BF16)
        h = _post(h, mix_a, mix_b, mod, None, g, w_out, w1, w2, 256)
        if want_ctx:
            hc = _post(hc, mix_a_c, mix_b_c, mod, ctx_row, g, w_out, w1, w2, ctx_len)
    return h
```
